```python
import math
import jax, jax.numpy as jnp
from jax import lax
import numpy as np

D_MODEL = 1024
BATCH = 8
SEQ = 2048
DEPTH = 4

DA_HEADS = 4
DA_QK_DIM = 64
DA_V_DIM = 2 * DA_QK_DIM
DA_WIDTH = DA_HEADS * DA_V_DIM
GDN_HEADS = 4
GDN_DIM = 128
GDN_WIDTH = GDN_HEADS * GDN_DIM
MIX_WIDTH = DA_WIDTH + GDN_WIDTH
CONV_K = 4
GDN_CHUNK = 64
Q_BLOCK = 128
DA_Q_COLS = DA_HEADS * 2 * DA_QK_DIM
DA_K_COLS = DA_HEADS * 2 * DA_QK_DIM
DA_V_COLS = DA_WIDTH
GDN_QKV_COLS = 3 * GDN_WIDTH
GDN_Z_COLS = GDN_WIDTH
IN_COLS = DA_Q_COLS + DA_K_COLS + DA_V_COLS + GDN_QKV_COLS + GDN_Z_COLS + 2 * GDN_HEADS
PEER_HEADS = 8
PEER_NKEYS = 128
PEER_N = PEER_NKEYS * PEER_NKEYS
PEER_QDIM = 256
PEER_HALF = PEER_QDIM // 2
PEER_TOPK = 16
PEER_TOKEN_BLOCK = 128
PLE_DIM = 256
EPS = 1e-6

kernel_name = "hybrid_diffattn_gdn_peer_ple"


def rms_norm(x, g):
    xf = x.astype(jnp.float32)
    y = xf * lax.rsqrt(jnp.mean(xf * xf, axis=-1, keepdims=True) + EPS)
    return (y * g.astype(jnp.float32)).astype(x.dtype)


def l2_normalize(x):
    return x * lax.rsqrt(jnp.sum(x * x, axis=-1, keepdims=True) + EPS)


def alibi_slopes(n):
    return jnp.exp2(-8.0 * (jnp.arange(n, dtype=jnp.float32) + 1.0) / n)


def diff_attention(q, k, v, lam, subln_g, lam_init):
    B, T = q.shape[0], q.shape[1]
    nb = T // Q_BLOCK
    scale = DA_QK_DIM ** -0.5
    qh = jnp.transpose(q, (0, 2, 3, 1, 4))
    kh = jnp.transpose(k, (0, 2, 3, 1, 4))
    vh = jnp.transpose(v, (0, 2, 1, 3))
    qb = jnp.moveaxis(qh.reshape(B, DA_HEADS, 2, nb, Q_BLOCK, DA_QK_DIM), 3, 0)
    slopes = alibi_slopes(DA_HEADS)[:, None, None, None]
    kpos = jnp.arange(T)

    def block(args):
        qblk, start = args
        qpos = start + jnp.arange(Q_BLOCK)
        dist = (qpos[:, None] - kpos[None, :]).astype(jnp.float32)
        s = jnp.einsum('bhcqd,bhckd->bhcqk', qblk, kh, preferred_element_type=jnp.float32) * scale
        s = jnp.where(dist >= 0, s - slopes * dist, -jnp.inf)
        a = jax.nn.softmax(s, axis=-1)
        w = a[:, :, 0] - lam * a[:, :, 1]
        return jnp.einsum('bhqk,bhkd->bhqd', w.astype(vh.dtype), vh)

    starts = jnp.arange(nb) * Q_BLOCK
    o = lax.map(block, (qb, starts))
    o = jnp.moveaxis(o, 0, 2).reshape(B, DA_HEADS, T, DA_V_DIM)
    o = jnp.transpose(o, (0, 2, 1, 3))
    o = rms_norm(o, subln_g) * (1.0 - lam_init)
    return o.reshape(B, T, DA_WIDTH)


def short_conv(x, w):
    T = x.shape[1]
    xp = jnp.pad(x, ((0, 0), (CONV_K - 1, 0), (0, 0)))
    y = xp[:, 0:T] * w[0]
    for j in range(1, CONV_K):
        y = y + xp[:, j:j + T] * w[j]
    return jax.nn.silu(y)


def chunk_gated_delta_rule(q, k, v, g, beta):
    B, T, H, dk = q.shape
    dv = v.shape[-1]
    C = GDN_CHUNK
    N = T // C

    def chunks(t):
        t = jnp.moveaxis(t, 2, 1)
        return t.reshape((B, H, N, C) + t.shape[3:])

    q, k, v, g, beta = chunks(q), chunks(k), chunks(v), chunks(g), chunks(beta)
    gc = jnp.cumsum(g, axis=-1)
    incl = jnp.tril(jnp.ones((C, C), dtype=bool))
    strict = jnp.tril(jnp.ones((C, C), dtype=bool), -1)
    decay = jnp.exp(jnp.where(incl, gc[..., :, None] - gc[..., None, :], -jnp.inf))
    kb = k * beta[..., None]
    L = jnp.where(strict, jnp.einsum('bhncd,bhnsd->bhncs', kb, k) * decay, 0.0)
    A = L + jnp.eye(C, dtype=jnp.float32)
    rhs = jnp.concatenate([v * beta[..., None], kb * jnp.exp(gc)[..., None]], axis=-1)
    sol = lax.linalg.triangular_solve(A, rhs, left_side=True, lower=True, unit_diagonal=True)
    u, w = sol[..., :dv], sol[..., dv:]
    intra = jnp.where(incl, jnp.einsum('bhncd,bhnsd->bhncs', q, k) * decay, 0.0)
    q_dec = q * jnp.exp(gc)[..., None]
    g_last = gc[..., -1]
    k_dec = k * jnp.exp(g_last[..., None] - gc)[..., None]

    def step(S, xs):
        qd, kd, ui, wi, ai, gl = xs
        v_new = ui - jnp.einsum('bhck,bhkv->bhcv', wi, S)
        o = jnp.einsum('bhck,bhkv->bhcv', qd, S) + jnp.einsum('bhcs,bhsv->bhcv', ai, v_new)
        S = S * jnp.exp(gl)[..., None, None] + jnp.einsum('bhck,bhcv->bhkv', kd, v_new)
        return S, o

    xs = (jnp.moveaxis(q_dec, 2, 0), jnp.moveaxis(k_dec, 2, 0), jnp.moveaxis(u, 2, 0),
          jnp.moveaxis(w, 2, 0), jnp.moveaxis(intra, 2, 0), jnp.moveaxis(g_last, 2, 0))
    S0 = jnp.zeros((B, H, dk, dv), jnp.float32)
    _, o = lax.scan(step, S0, xs)
    o = jnp.moveaxis(o, 0, 2).reshape(B, H, T, dv)
    return jnp.moveaxis(o, 1, 2)


def gated_deltanet(qkv, z, b, a, conv_w, a_log, dt_bias, norm_g):
    B, T, _ = qkv.shape
    dtype = qkv.dtype
    qkv = short_conv(qkv, conv_w).astype(jnp.float32)
    q = qkv[..., :GDN_WIDTH].reshape(B, T, GDN_HEADS, GDN_DIM)
    k = qkv[..., GDN_WIDTH:2 * GDN_WIDTH].reshape(B, T, GDN_HEADS, GDN_DIM)
    v = qkv[..., 2 * GDN_WIDTH:].reshape(B, T, GDN_HEADS, GDN_DIM)
    q = l2_normalize(q) * (GDN_DIM ** -0.5)
    k = l2_normalize(k)
    beta = jax.nn.sigmoid(b.astype(jnp.float32))
    g = -jnp.exp(a_log.astype(jnp.float32)) * jax.nn.softplus(a.astype(jnp.float32) + dt_bias.astype(jnp.float32))
    o = chunk_gated_delta_rule(q, k, v, g, beta)
    o = rms_norm(o, norm_g) * jax.nn.silu(z.reshape(B, T, GDN_HEADS, GDN_DIM).astype(jnp.float32))
    return o.reshape(B, T, GDN_WIDTH).astype(dtype)


def peer(xn, wq, subkeys, u_tab, v_tab):
    B, T, D = xn.shape
    K = PEER_TOPK
    q = (xn @ wq).reshape(B, T, PEER_HEADS, 2, PEER_HALF)
    s = jnp.einsum('bthcd,hcnd->bthcn', q, subkeys, preferred_element_type=jnp.float32)
    v1, i1 = lax.top_k(s[..., 0, :], K)
    v2, i2 = lax.top_k(s[..., 1, :], K)
    cand = (v1[..., :, None] + v2[..., None, :]).reshape(B, T, PEER_HEADS, K * K)
    cidx = (i1[..., :, None] * PEER_NKEYS + i2[..., None, :]).reshape(B, T, PEER_HEADS, K * K)
    top_v, top_p = lax.top_k(cand, K)
    idx = jnp.take_along_axis(cidx, top_p, axis=-1)
    gate = jax.nn.softmax(top_v, axis=-1)
    M = B * T
    nb = M // PEER_TOKEN_BLOCK
    E = PEER_HEADS * K
    xs = (xn.reshape(nb, PEER_TOKEN_BLOCK, D), idx.reshape(nb, PEER_TOKEN_BLOCK, E),
          gate.reshape(nb, PEER_TOKEN_BLOCK, E).astype(xn.dtype))

    def block(args):
        xb, ib, gb = args
        h = jnp.einsum('md,med->me', xb, u_tab[ib])
        return jnp.einsum('me,med->md', gb * jax.nn.gelu(h, approximate=False), v_tab[ib])

    y = lax.map(block, xs)
    return y.reshape(B, T, D)


def setup_inputs(seed: int = 0) -> dict:
    key = jax.random.key(seed)
    ks = jax.random.split(key, 24)
    f32 = jnp.float32
    nrm = lambda k, shape, s: jax.random.normal(k, shape, f32) * s
    gain = lambda k, shape: 1.0 + 0.02 * jax.random.normal(k, shape, f32)
    A = jax.random.uniform(ks[5], (DEPTH, GDN_HEADS), f32, 1.0, 16.0)
    dt = jnp.exp(jax.random.uniform(ks[6], (DEPTH, GDN_HEADS), f32, math.log(1e-3), math.log(1e-1)))
    return {
        "x": nrm(ks[0], (BATCH, SEQ, D_MODEL), 1.0),
        "p": nrm(ks[1], (DEPTH, BATCH, SEQ, PLE_DIM), 1.0),
        "ln_mix_g": gain(ks[2], (DEPTH, D_MODEL)),
        "w_in": nrm(ks[3], (DEPTH, D_MODEL, IN_COLS), D_MODEL ** -0.5),
        "conv_w": nrm(ks[4], (DEPTH, CONV_K, GDN_QKV_COLS), CONV_K ** -0.5),
        "gdn_a_log": jnp.log(A),
        "gdn_dt_bias": dt + jnp.log(-jnp.expm1(-dt)),
        "gdn_norm_g": gain(ks[7], (DEPTH, GDN_DIM)),
        "diff_lambda": nrm(ks[8], (DEPTH, 4, DA_QK_DIM), 0.1),
        "diff_subln_g": gain(ks[9], (DEPTH, DA_V_DIM)),
        "w_out": nrm(ks[10], (DEPTH, MIX_WIDTH, D_MODEL), MIX_WIDTH ** -0.5),
        "ln_ffn_g": gain(ks[11], (DEPTH, D_MODEL)),
        "peer_wq": nrm(ks[12], (DEPTH, D_MODEL, PEER_HEADS * PEER_QDIM), D_MODEL ** -0.5),
        "peer_subkeys": nrm(ks[13], (DEPTH, PEER_HEADS, 2, PEER_NKEYS, PEER_HALF), PEER_HALF ** -0.5),
        "peer_u": nrm(ks[14], (DEPTH, PEER_N, D_MODEL), D_MODEL ** -0.5),
        "peer_v": nrm(ks[15], (DEPTH, PEER_N, D_MODEL), 0.25),
        "ple_wp": nrm(ks[16], (DEPTH, PLE_DIM, D_MODEL), PLE_DIM ** -0.5),
        "ple_wg": nrm(ks[17], (DEPTH, D_MODEL, D_MODEL), D_MODEL ** -0.5),
        "ple_norm_g": gain(ks[18], (DEPTH, D_MODEL)),
        "final_norm_g": gain(ks[19], (D_MODEL,)),
    }


def reference(x, p, ln_mix_g, w_in, conv_w, gdn_a_log, gdn_dt_bias, gdn_norm_g, diff_lambda,
              diff_subln_g, w_out, ln_ffn_g, peer_wq, peer_subkeys, peer_u, peer_v,
              ple_wp, ple_wg, ple_norm_g, final_norm_g):
    B, T, _ = x.shape
    h = x
    for i in range(DEPTH):
        xn = rms_norm(h, ln_mix_g[i])
        proj = xn @ w_in[i]
        o0 = 0
        dq = proj[..., o0:o0 + DA_Q_COLS]; o0 += DA_Q_COLS
        dk = proj[..., o0:o0 + DA_K_COLS]; o0 += DA_K_COLS
        dv = proj[..., o0:o0 + DA_V_COLS]; o0 += DA_V_COLS
        gqkv = proj[..., o0:o0 + GDN_QKV_COLS]; o0 += GDN_QKV_COLS
        gz = proj[..., o0:o0 + GDN_Z_COLS]; o0 += GDN_Z_COLS
        gb = proj[..., o0:o0 + GDN_HEADS]; o0 += GDN_HEADS
        ga = proj[..., o0:o0 + GDN_HEADS]

        lam_init = 0.8 - 0.6 * math.exp(-0.3 * i)
        lp = diff_lambda[i].astype(jnp.float32)
        lam = jnp.exp(jnp.sum(lp[0] * lp[1])) - jnp.exp(jnp.sum(lp[2] * lp[3])) + lam_init
        o_da = diff_attention(dq.reshape(B, T, DA_HEADS, 2, DA_QK_DIM),
                              dk.reshape(B, T, DA_HEADS, 2, DA_QK_DIM),
                              dv.reshape(B, T, DA_HEADS, DA_V_DIM),
                              lam, diff_subln_g[i], lam_init)
        o_gdn = gated_deltanet(gqkv, gz, gb, ga, conv_w[i], gdn_a_log[i], gdn_dt_bias[i], gdn_norm_g[i])
        h = h + jnp.concatenate([o_da, o_gdn], axis=-1) @ w_out[i]

        h = h + peer(rms_norm(h, ln_ffn_g[i]), peer_wq[i], peer_subkeys[i], peer_u[i], peer_v[i])

        e = rms_norm(p[i] @ ple_wp[i], ple_norm_g[i])
        h = h + jax.nn.sigmoid(h @ ple_wg[i]) * e
    return rms_norm(h, final_norm_g)
```

```python
import functools
import math

import jax
import jax.numpy as jnp
from jax import lax
from jax.experimental import pallas as pl
from jax.experimental.pallas import tpu as pltpu

F32 = jnp.float32
BF16 = jnp.bfloat16
EPS = 1e-6

DA_HEADS = 4
DA_QK = 64
DA_V = 2 * DA_QK
DA_WIDTH = DA_HEADS * DA_V
GDN_HEADS = 4
GDN_DIM = 128
GDN_WIDTH = GDN_HEADS * GDN_DIM
CONV_K = 4
GDN_CHUNK = 64
PEER_HEADS = 8
PEER_NKEYS = 128
PEER_HALF = 128
PEER_TOPK = 16
LANES = 128
CONV_HALO = 8

NT_DIMS = (((1,), (1,)), ((), ()))
TN_DIMS = (((0,), (0,)), ((), ()))


def _params(semantics, vmem_mb):
    return pltpu.CompilerParams(dimension_semantics=semantics, vmem_limit_bytes=vmem_mb * 1024 * 1024)


def _rms(x, g):
    return x * lax.rsqrt(jnp.mean(x * x, axis=-1, keepdims=True) + EPS) * g


def _row_tile(m, want):
    t = min(want, m)
    assert m % t == 0, (m, t)
    return t


def _inproj_body(h_ref, g_ref, wda_ref, wqkv_ref, wz_ref, wba_ref, da_ref, gq_ref, gz_ref, gba_ref):
    xn = _rms(h_ref[...], g_ref[...]).astype(BF16)
    da_ref[...] = jnp.dot(xn, wda_ref[...], preferred_element_type=F32).astype(BF16)
    gq_ref[...] = jnp.dot(xn, wqkv_ref[...], preferred_element_type=F32)
    gz_ref[...] = jnp.dot(xn, wz_ref[...], preferred_element_type=F32)
    gba_ref[...] = jnp.dot(xn, wba_ref[...], preferred_element_type=F32)


def _inproj(h, g, wda, wqkv, wz, wba):
    m, d = h.shape
    tm = _row_tile(m, 256)
    full = lambda a: pl.BlockSpec(a.shape, lambda i: (0, 0))
    rows = lambda n: pl.BlockSpec((tm, n), lambda i: (i, 0))
    return pl.pallas_call(
        _inproj_body,
        grid=(m // tm,),
        in_specs=[rows(d), full(g), full(wda), full(wqkv), full(wz), full(wba)],
        out_specs=[rows(wda.shape[1]), rows(wqkv.shape[1]), rows(wz.shape[1]), rows(wba.shape[1])],
        out_shape=[
            jax.ShapeDtypeStruct((m, wda.shape[1]), BF16),
            jax.ShapeDtypeStruct((m, wqkv.shape[1]), F32),
            jax.ShapeDtypeStruct((m, wz.shape[1]), F32),
            jax.ShapeDtypeStruct((m, wba.shape[1]), F32),
        ],
        compiler_params=_params(("parallel",), 48),
        name="inproj",
    )(h, g, wda, wqkv, wz, wba)


def _attn_body(slope_ref, lam_ref, g_ref, q_ref, k_ref, v_ref, o_ref, *, lam_init, tq):
    hd = pl.program_id(1)
    qi = pl.program_id(2)
    slope = slope_ref[hd]
    lp = lam_ref[...]
    lam = (jnp.exp(jnp.sum(lp[0:1] * lp[1:2], axis=1, keepdims=True))
           - jnp.exp(jnp.sum(lp[2:3] * lp[3:4], axis=1, keepdims=True)) + lam_init)
    q = q_ref[...]
    scale = DA_QK ** -0.5
    rel = (lax.broadcasted_iota(jnp.int32, (tq, tq), 0) - lax.broadcasted_iota(jnp.int32, (tq, tq), 1))

    def kv_step(j, carry):
        start = pl.multiple_of(j * tq, tq)
        kb = k_ref[pl.ds(start, tq), :]
        vb = v_ref[pl.ds(start, tq), :]
        dist = (rel + (qi - j) * tq).astype(F32)
        bias = slope * dist
        valid = dist >= 0
        out = []
        for c in range(2):
            m, l, acc = carry[3 * c:3 * c + 3]
            s = lax.dot_general(q[:, c * DA_QK:(c + 1) * DA_QK], kb[:, c * DA_QK:(c + 1) * DA_QK], NT_DIMS,
                                preferred_element_type=F32) * scale
            s = jnp.where(valid, s - bias, -jnp.inf)
            m_new = jnp.maximum(m, jnp.max(s, axis=1, keepdims=True))
            alpha = jnp.exp(m - m_new)
            p = jnp.exp(s - m_new)
            l = alpha * l + jnp.sum(p, axis=1, keepdims=True)
            acc = alpha * acc + jnp.dot(p.astype(BF16), vb, preferred_element_type=F32)
            out += [m_new, l, acc]
        return tuple(out)

    init = (jnp.full((tq, 1), -jnp.inf, F32), jnp.zeros((tq, 1), F32), jnp.zeros((tq, DA_V), F32)) * 2
    m0, l0, a0, m1, l1, a1 = lax.fori_loop(0, qi + 1, kv_step, init)
    o = a0 / l0 - lam * (a1 / l1)
    o_ref[...] = (_rms(o, g_ref[...]) * (1.0 - lam_init)).astype(o_ref.dtype)


def _diff_attn(da, slopes, lam_p, subln_g, lam_init, b, t):
    tq = _row_tile(t, 256)
    nh = DA_HEADS
    return pl.pallas_call(
        functools.partial(_attn_body, lam_init=lam_init, tq=tq),
        grid=(b, nh, t // tq),
        in_specs=[
            pl.BlockSpec(memory_space=pltpu.SMEM),
            pl.BlockSpec(lam_p.shape, lambda bi, h, qi: (0, 0)),
            pl.BlockSpec(subln_g.shape, lambda bi, h, qi: (0, 0)),
            pl.BlockSpec((None, tq, DA_V), lambda bi, h, qi: (bi, qi, h)),
            pl.BlockSpec((None, t, DA_V), lambda bi, h, qi: (bi, 0, nh + h)),
            pl.BlockSpec((None, t, DA_V), lambda bi, h, qi: (bi, 0, 2 * nh + h)),
        ],
        out_specs=pl.BlockSpec((None, tq, DA_V), lambda bi, h, qi: (bi, qi, h)),
        out_shape=jax.ShapeDtypeStruct((b, t, DA_WIDTH), BF16),
        compiler_params=_params(("parallel", "parallel", "arbitrary"), 32),
        name="diff_attn",
    )(slopes, lam_p, subln_g, da, da, da)


def _gdn_prep_body(prev_ref, cur_ref, cw_ref, gba_ref, alog_ref, dtb_ref, q_ref, k_ref, v_ref, gt_ref, *, tr):
    r = pl.program_id(1)
    cw = cw_ref[...]
    w = GDN_WIDTH
    for part, dst in enumerate((q_ref, k_ref, v_ref)):
        cols = slice(part * w, (part + 1) * w)
        prev = jnp.where(r > 0, prev_ref[:, cols], 0.0)
        xc = jnp.concatenate([prev, cur_ref[:, cols]], axis=0)
        y = xc[CONV_HALO - 3:CONV_HALO - 3 + tr] * cw[0:1, cols]
        for j in range(1, CONV_K):
            y = y + xc[CONV_HALO - 3 + j:CONV_HALO - 3 + j + tr] * cw[j:j + 1, cols]
        y = y * jax.nn.sigmoid(y)
        if part < 2:
            post = GDN_DIM ** -0.5 if part == 0 else 1.0
            for h in range(GDN_HEADS):
                yh = y[:, h * GDN_DIM:(h + 1) * GDN_DIM]
                yh = yh * lax.rsqrt(jnp.sum(yh * yh, axis=-1, keepdims=True) + EPS)
                dst[:, h * GDN_DIM:(h + 1) * GDN_DIM] = yh * post if part == 0 else yh
        else:
            dst[...] = y
    gba = gba_ref[...]
    lane = lax.broadcasted_iota(jnp.int32, gba.shape, 1)
    beta = jax.nn.sigmoid(gba)
    g = -jnp.exp(alog_ref[...]) * jax.nn.softplus(gba + dtb_ref[...])
    ii = lax.broadcasted_iota(jnp.int32, (tr, tr), 0)
    jj = lax.broadcasted_iota(jnp.int32, (tr, tr), 1)
    tri = jnp.where((jj <= ii) & ((ii >> 6) == (jj >> 6)), 1.0, 0.0).astype(F32)
    gc = jnp.dot(tri, jnp.where((lane >= GDN_HEADS) & (lane < 2 * GDN_HEADS), g, 0.0),
                 precision=lax.Precision.HIGHEST, preferred_element_type=F32)
    lane_h = lax.broadcasted_iota(jnp.int32, (tr, GDN_DIM), 1)
    for h in range(GDN_HEADS):
        bh = jnp.broadcast_to(beta[:, h:h + 1], (tr, GDN_DIM))
        ch = jnp.broadcast_to(gc[:, GDN_HEADS + h:GDN_HEADS + h + 1], (tr, GDN_DIM))
        gt_ref[:, h * GDN_DIM:(h + 1) * GDN_DIM] = jnp.where(lane_h == 0, bh, jnp.where(lane_h == 1, ch, 0.0))


def _gdn_prep(gq, gba, conv_w, alog_row, dtb_row, b, t):
    tr = _row_tile(t, 256)
    w3 = 3 * GDN_WIDTH
    out_sds = jax.ShapeDtypeStruct((b, t, GDN_WIDTH), F32)
    blk = lambda n: pl.BlockSpec((None, tr, n), lambda bi, r: (bi, r, 0))
    return pl.pallas_call(
        functools.partial(_gdn_prep_body, tr=tr),
        grid=(b, t // tr),
        in_specs=[
            pl.BlockSpec((None, CONV_HALO, w3), lambda bi, r: (bi, jnp.maximum(r * (tr // CONV_HALO) - 1, 0), 0)),
            blk(w3),
            pl.BlockSpec(conv_w.shape, lambda bi, r: (0, 0)),
            blk(LANES),
            pl.BlockSpec(alog_row.shape, lambda bi, r: (0, 0)),
            pl.BlockSpec(dtb_row.shape, lambda bi, r: (0, 0)),
        ],
        out_specs=[blk(GDN_WIDTH)] * 4,
        out_shape=[out_sds] * 4,
        compiler_params=_params(("parallel", "parallel"), 48),
        name="gdn_prep",
    )(gq, gq, conv_w, gba, alog_row, dtb_row)


def _unit_lower_inverse(lmat, ii, jj):
    hi = lax.Precision.HIGHEST
    mm = lambda a, b: jnp.dot(a, b, precision=hi, preferred_element_type=F32)
    same16 = (ii >> 4) == (jj >> 4)
    same32 = (ii >> 5) == (jj >> 5)
    eye = jnp.where(ii == jj, 1.0, 0.0).astype(F32)
    ld = jnp.where(same16, lmat, 0.0)
    p = eye - ld
    q = mm(ld, ld)
    p = p + mm(p, q)
    q = mm(q, q)
    p = p + mm(p, q)
    q = mm(q, q)
    p = p + mm(p, q)
    m1 = jnp.where(same32 & jnp.logical_not(same16), lmat, 0.0)
    x1 = p - mm(p, mm(m1, p))
    m2 = jnp.where(same32, 0.0, lmat)
    return x1 - mm(x1, mm(m2, x1))


def _gdn_main_body(q_ref, k_ref, v_ref, gt_ref, z_ref, ng_ref, o_ref, u_s, w_s, qd_s, kd_s, in_s, *, t):
    c = GDN_CHUNK
    n_chunks = t // c
    ii = lax.broadcasted_iota(jnp.int32, (c, c), 0)
    jj = lax.broadcasted_iota(jnp.int32, (c, c), 1)
    hi = lax.Precision.HIGHEST

    def chunk_prep(n, _):
        sl = pl.ds(pl.multiple_of(n * c, c), c)
        q = q_ref[sl, :]
        k = k_ref[sl, :]
        v = v_ref[sl, :]
        gt = gt_ref[sl, :]
        beta = gt[:, 0:1]
        gc = gt[:, 1:2]
        g_last = gc[c - 1:c, :]
        gc_row = jnp.transpose(jnp.broadcast_to(gc, (c, GDN_DIM)))[0:c, :]
        decay = jnp.exp(jnp.where(jj <= ii, gc - gc_row, -jnp.inf))
        kb = k * beta
        k16 = k.astype(BF16)
        kk = lax.dot_general(kb.astype(BF16), k16, NT_DIMS, preferred_element_type=F32)
        lmat = jnp.where(jj < ii, kk * decay, 0.0)
        tinv = _unit_lower_inverse(lmat, ii, jj)
        egc = jnp.exp(gc)
        rhs = jnp.concatenate([v * beta, kb * egc], axis=1)
        sol = jnp.dot(tinv, rhs, precision=hi, preferred_element_type=F32)
        u_s[sl, :] = sol[:, :GDN_DIM]
        w_s[sl, :] = sol[:, GDN_DIM:]
        qk = lax.dot_general(q.astype(BF16), k16, NT_DIMS, preferred_element_type=F32)
        in_s[sl, :] = jnp.where(jj <= ii, qk * decay, 0.0)
        qd_s[sl, :] = q * egc
        kd_s[sl, :] = k * jnp.exp(g_last - gc)
        return 0

    lax.fori_loop(0, n_chunks, chunk_prep, 0)

    def chunk_scan(n, s):
        sl = pl.ds(pl.multiple_of(n * c, c), c)
        s16 = s.astype(BF16)
        g_last = gt_ref[sl, :][c - 1:c, 1:2]
        v_new = u_s[sl, :] - jnp.dot(w_s[sl, :].astype(BF16), s16, preferred_element_type=F32)
        vn16 = v_new.astype(BF16)
        o = (jnp.dot(qd_s[sl, :].astype(BF16), s16, preferred_element_type=F32)
             + jnp.dot(in_s[sl, :].astype(BF16), vn16, preferred_element_type=F32))
        s = s * jnp.exp(g_last) + lax.dot_general(kd_s[sl, :].astype(BF16), vn16, TN_DIMS,
                                                  preferred_element_type=F32)
        z = z_ref[sl, :]
        o_ref[sl, :] = (_rms(o, ng_ref[...]) * (z * jax.nn.sigmoid(z))).astype(o_ref.dtype)
        return s

    lax.fori_loop(0, n_chunks, chunk_scan, jnp.zeros((GDN_DIM, GDN_DIM), F32))


def _gdn_main(q, k, v, gt, z, norm_g, b, t):
    blk = pl.BlockSpec((None, t, GDN_DIM), lambda bi, h: (bi, 0, h))
    return pl.pallas_call(
        functools.partial(_gdn_main_body, t=t),
        grid=(b, GDN_HEADS),
        in_specs=[blk, blk, blk, blk, blk, pl.BlockSpec(norm_g.shape, lambda bi, h: (0, 0))],
        out_specs=blk,
        out_shape=jax.ShapeDtypeStruct((b, t, GDN_WIDTH), BF16),
        scratch_shapes=[pltpu.VMEM((t, GDN_DIM), F32)] * 4 + [pltpu.VMEM((t, GDN_CHUNK), F32)],
        compiler_params=_params(("parallel", "parallel"), 48),
        name="gdn_main",
    )(q, k, v, gt, z, norm_g)


def _outproj_body(h_ref, oda_ref, ogdn_ref, wo_ref, g_ref, wq_ref, h1_ref, xn_ref, pq_ref):
    h1 = (h_ref[...]
          + jnp.dot(oda_ref[...], wo_ref[0:DA_WIDTH, :], preferred_element_type=F32)
          + jnp.dot(ogdn_ref[...], wo_ref[DA_WIDTH:, :], preferred_element_type=F32))
    h1_ref[...] = h1
    xn = _rms(h1, g_ref[...]).astype(BF16)
    xn_ref[...] = xn
    pq_ref[...] = jnp.dot(xn, wq_ref[...], preferred_element_type=F32).astype(BF16)


def _outproj(h, oda, ogdn, wo, g, wq):
    m, d = h.shape
    tm = _row_tile(m, 256)
    full = lambda a: pl.BlockSpec(a.shape, lambda i: (0, 0))
    rows = lambda n: pl.BlockSpec((tm, n), lambda i: (i, 0))
    nq = wq.shape[1]
    return pl.pallas_call(
        _outproj_body,
        grid=(m // tm,),
        in_specs=[rows(d), rows(oda.shape[1]), rows(ogdn.shape[1]), full(wo), full(g), full(wq)],
        out_specs=[rows(d), rows(d), rows(nq)],
        out_shape=[jax.ShapeDtypeStruct((m, d), F32), jax.ShapeDtypeStruct((m, d), BF16),
                   jax.ShapeDtypeStruct((m, nq), BF16)],
        compiler_params=_params(("parallel",), 48),
        name="outproj_peerq",
    )(h, oda, ogdn, wo, g, wq)


def _top_rows(s, rounds):
    n, width = s.shape
    row = lax.broadcasted_iota(jnp.int32, (n, width), 0)
    slot = lax.broadcasted_iota(jnp.int32, (rounds, width), 0)

    def rnd(r, carry):
        work, rank, vals = carry
        m = jnp.max(work, axis=0, keepdims=True)
        first = jnp.min(jnp.where(work == m, row, n), axis=0, keepdims=True)
        sel = row == first
        rank = jnp.where(sel, r.astype(F32), rank)
        work = jnp.where(sel, -jnp.inf, work)
        vals = jnp.where(slot == r, m, vals)
        return work, rank, vals

    init = (s, jnp.full((n, width), float(rounds), F32), jnp.zeros((rounds, width), F32))
    _, rank, vals = lax.fori_loop(0, rounds, rnd, init)
    return rank, vals


def _route_body(pq_ref, sk_ref, r2_ref, e2_ref, c_ref, e1_ref):
    kk = PEER_TOPK
    q = pq_ref[...]
    s1 = lax.dot_general(sk_ref[0], q[:, :PEER_HALF], NT_DIMS, preferred_element_type=F32)
    s2 = lax.dot_general(sk_ref[1], q[:, PEER_HALF:], NT_DIMS, preferred_element_type=F32)
    rank1, v1 = _top_rows(s1, kk)
    rank2, v2 = _top_rows(s2, kk)
    width = s1.shape[1]
    cand = jnp.concatenate([v1[i:i + 1] + v2 for i in range(kk)], axis=0)
    row = lax.broadcasted_iota(jnp.int32, cand.shape, 0)
    slot = lax.broadcasted_iota(jnp.int32, (kk, width), 0)
    top = v1[0:1] + v2[0:1]

    def rnd(r, carry):
        work, cnt, zsum = carry
        m = jnp.max(work, axis=0, keepdims=True)
        first = jnp.min(jnp.where(work == m, row, kk * kk), axis=0, keepdims=True)
        work = jnp.where(row == first, -jnp.inf, work)
        cnt = cnt + jnp.where(slot == (first >> 4), 1.0, 0.0)
        return work, cnt, zsum + jnp.exp(m - top)

    _, cnt, zsum = lax.fori_loop(0, kk, rnd, (cand, jnp.zeros((kk, width), F32), jnp.zeros((1, width), F32)))
    csel = jnp.zeros_like(s1)
    for i in range(kk):
        csel = csel + jnp.where(rank1 == float(i), cnt[i:i + 1], 0.0)
    c_ref[...] = csel
    e1_ref[...] = jnp.exp(s1 - v1[0:1]) / zsum
    r2_ref[...] = rank2.astype(BF16)
    e2_ref[...] = jnp.exp(s2 - v2[0:1]).astype(BF16)


def _peer_route(pq, sk):
    m = pq.shape[0]
    tb = _row_tile(m, 128)
    nh = PEER_HEADS
    out_blk = pl.BlockSpec((None, PEER_NKEYS, tb), lambda i, h: (h, 0, i))
    sds = lambda dt: jax.ShapeDtypeStruct((nh, PEER_NKEYS, m), dt)
    return pl.pallas_call(
        _route_body,
        grid=(m // tb, nh),
        in_specs=[
            pl.BlockSpec((tb, 2 * PEER_HALF), lambda i, h: (i, h)),
            pl.BlockSpec((None, 2, PEER_NKEYS, PEER_HALF), lambda i, h: (h, 0, 0, 0)),
        ],
        out_specs=[out_blk] * 4,
        out_shape=[sds(BF16), sds(BF16), sds(F32), sds(F32)],
        compiler_params=_params(("parallel", "parallel"), 32),
        name="peer_route",
    )(pq, sk)


def _peer_main_body(h_ref, xn_ref, u_ref, vt_ref, r2_ref, e2_ref, c_ref, e1_ref, o_ref, acc_ref, w_ref, *, te):
    j = pl.program_id(1)
    nk = PEER_NKEYS

    @pl.when(j == 0)
    def _():
        acc_ref[...] = jnp.zeros_like(acc_ref)

    ht = lax.dot_general(u_ref[...], xn_ref[...], NT_DIMS, preferred_element_type=F32)
    for aa in range(te // nk):
        a = j * (te // nk) + aa
        x = ht[aa * nk:(aa + 1) * nk, :]
        act = (0.5 * x * (1.0 + lax.erf(x * (2.0 ** -0.5)))).astype(BF16)
        gate = None
        for h in range(PEER_HEADS):
            cnt = c_ref[h, pl.ds(a, 1), :].astype(BF16)
            e1 = e1_ref[h, pl.ds(a, 1), :].astype(BF16)
            e2 = e2_ref[h]
            term = jnp.where(r2_ref[h] < cnt, e2, jnp.zeros_like(e2)) * e1
            gate = term if gate is None else gate + term
        w_ref[aa * nk:(aa + 1) * nk, :] = act * gate
    acc_ref[...] += jnp.dot(vt_ref[...], w_ref[...], preferred_element_type=F32)

    @pl.when(j == pl.num_programs(1) - 1)
    def _():
        o_ref[...] = h_ref[...] + jnp.transpose(acc_ref[...])


def _peer_main(h1, xn, u, vt, r2, e2, cs, e1):
    m, d = h1.shape
    n = u.shape[0]
    tb = _row_tile(m, 512)
    te = 512
    nh = PEER_HEADS
    gate_blk = pl.BlockSpec((nh, PEER_NKEYS, tb), lambda i, j: (0, 0, i))
    return pl.pallas_call(
        functools.partial(_peer_main_body, te=te),
        grid=(m // tb, n // te),
        in_specs=[
            pl.BlockSpec((tb, d), lambda i, j: (i, 0)),
            pl.BlockSpec((tb, d), lambda i, j: (i, 0)),
            pl.BlockSpec((te, d), lambda i, j: (j, 0)),
            pl.BlockSpec((d, te), lambda i, j: (0, j)),
            gate_blk, gate_blk, gate_blk, gate_blk,
        ],
        out_specs=pl.BlockSpec((tb, d), lambda i, j: (i, 0)),
        out_shape=jax.ShapeDtypeStruct((m, d), F32),
        scratch_shapes=[pltpu.VMEM((d, tb), F32), pltpu.VMEM((te, tb), BF16)],
        compiler_params=_params(("parallel", "arbitrary"), 48),
        name="peer_main",
    )(h1, xn, u, vt, r2, e2, cs, e1)


def _ple_body(h_ref, p_ref, wp_ref, wg_ref, ng_ref, fg_ref, o_ref, *, final):
    h = h_ref[...]
    e = _rms(jnp.dot(p_ref[...].astype(BF16), wp_ref[...], preferred_element_type=F32), ng_ref[...])
    gate = jax.nn.sigmoid(jnp.dot(h.astype(BF16), wg_ref[...], preferred_element_type=F32))
    out = h + gate * e
    if final:
        out = _rms(out, fg_ref[...])
    o_ref[...] = out


def _ple(h, p, wp, wg, ng, fg, final):
    m, d = h.shape
    tm = _row_tile(m, 256)
    full = lambda a: pl.BlockSpec(a.shape, lambda i: (0, 0))
    rows = lambda n: pl.BlockSpec((tm, n), lambda i: (i, 0))
    return pl.pallas_call(
        functools.partial(_ple_body, final=final),
        grid=(m // tm,),
        in_specs=[rows(d), rows(p.shape[1]), full(wp), full(wg), full(ng), full(fg)],
        out_specs=rows(d),
        out_shape=jax.ShapeDtypeStruct((m, d), F32),
        compiler_params=_params(("parallel",), 32),
        name="ple",
    )(h, p, wp, wg, ng, fg)


def _pad_lanes(vec, offset):
    return jnp.zeros((1, LANES), F32).at[0, offset:offset + vec.shape[0]].set(vec.astype(F32))


def kernel(x, p, ln_mix_g, w_in, conv_w, gdn_a_log, gdn_dt_bias, gdn_norm_g, diff_lambda, diff_subln_g,
           w_out, ln_ffn_g, peer_wq, peer_subkeys, peer_u, peer_v, ple_wp, ple_wg, ple_norm_g, final_norm_g):
    b, t, d = x.shape
    depth = w_in.shape[0]
    m = b * t
    h = x.reshape(m, d).astype(F32)
    slopes = jnp.exp2(-8.0 * (jnp.arange(DA_HEADS, dtype=F32) + 1.0) / DA_HEADS)
    row = lambda v: v.reshape(1, -1).astype(F32)
    c0 = 3 * DA_WIDTH
    c1 = c0 + 3 * GDN_WIDTH
    c2 = c1 + GDN_WIDTH
    for i in range(depth):
        lam_init = 0.8 - 0.6 * math.exp(-0.3 * i)
        wi = w_in[i]
        wba = jnp.zeros((d, LANES), BF16).at[:, :2 * GDN_HEADS].set(wi[:, c2:].astype(BF16))
        da, gq, gz, gba = _inproj(h, row(ln_mix_g[i]), wi[:, :c0].astype(BF16), wi[:, c0:c1].astype(BF16),
                                  wi[:, c1:c2].astype(BF16), wba)
        o_da = _diff_attn(da.reshape(b, t, c0), slopes, diff_lambda[i].astype(F32), row(diff_subln_g[i]),
                          lam_init, b, t)
        gq_, gk_, gv_, gt = _gdn_prep(gq.reshape(b, t, -1), gba.reshape(b, t, LANES), conv_w[i].astype(F32),
                                      _pad_lanes(gdn_a_log[i], GDN_HEADS), _pad_lanes(gdn_dt_bias[i], GDN_HEADS),
                                      b, t)
        o_gdn = _gdn_main(gq_, gk_, gv_, gt, gz.reshape(b, t, GDN_WIDTH), row(gdn_norm_g[i]), b, t)
        h1, xn, pq = _outproj(h, o_da.reshape(m, DA_WIDTH), o_gdn.reshape(m, GDN_WIDTH), w_out[i].astype(BF16),
                              row(ln_ffn_g[i]), peer_wq[i].astype(BF16))
        r2, e2, cs, e1 = _peer_route(pq, peer_subkeys[i].astype(BF16))
        h2 = _peer_main(h1, xn, peer_u[i].astype(BF16), jnp.transpose(peer_v[i]).astype(BF16), r2, e2, cs, e1)
        h = _ple(h2, p[i].reshape(m, -1), ple_wp[i].astype(BF16), ple_wg[i].astype(BF16), row(ple_norm_g[i]),
                 row(final_norm_g), final=(i == depth - 1))
    return h.reshape(b, t, d).astype(x.dtype)
```
